```python
import math
import jax, jax.numpy as jnp
from jax import lax
import numpy as np

D_MODEL = 1024
BATCH = 4
SEQ = 4096
DEPTH = 4
DEC_BATCH = 32
DEC_SEQ = 4
PAST_LEN = 8192
PAGE_SIZE = 128

N_SB_HEADS = 8
SB_HEAD_DIM = 64
SB_WIDTH = N_SB_HEADS * SB_HEAD_DIM
N_DF_HEADS = 4
DF_QK_DIM = 64
DF_V_DIM = 2 * DF_QK_DIM
DF_QK_WIDTH = N_DF_HEADS * 2 * DF_QK_DIM
DF_WIDTH = N_DF_HEADS * DF_V_DIM
MIX_WIDTH = SB_WIDTH + DF_WIDTH
IN_WIDTH = 3 * SB_WIDTH + 2 * DF_QK_WIDTH + DF_WIDTH
D_FF = 4 * D_MODEL
Q_BLOCK = 128
NORM_EPS = 1e-6
SUBLN_EPS = 1e-5
NEG_INF = -1e30
N_MOD = 6

kernel_name = 'stickbreak_diffattn_hybrid_adaln_step'


def rms(x, eps=NORM_EPS):
    x32 = x.astype(jnp.float32)
    return x32 * lax.rsqrt(jnp.mean(x32 * x32, axis=-1, keepdims=True) + eps)


def to_blocks(q, q_pos):
    b, tq = q.shape[0], q.shape[1]
    rest = q.shape[2:]
    blk = min(Q_BLOCK, tq)
    nb = -(-tq // blk)
    pad = nb * blk - tq
    q = jnp.pad(q, [(0, 0), (0, pad)] + [(0, 0)] * len(rest))
    qp = jnp.concatenate([q_pos, jnp.full((pad,), q_pos[-1], q_pos.dtype)])
    qb = jnp.moveaxis(q.reshape((b, nb, blk) + rest), 1, 0)
    return qb, qp.reshape(nb, blk), tq


def from_blocks(o, tq):
    nb, b, blk = o.shape[:3]
    return jnp.moveaxis(o, 0, 1).reshape((b, nb * blk) + o.shape[3:])[:, :tq]


def stick_breaking_attention(q, k, v, q_pos, k_pos):
    scale = SB_HEAD_DIM ** -0.5
    qb, pb, tq = to_blocks(q, q_pos)

    def one_block(args):
        qi, pi = args
        z = jnp.einsum('bqhd,bkhd->bhqk', qi, k).astype(jnp.float32) * scale
        causal = k_pos[None, :] < pi[:, None]
        log_keep = jnp.where(causal, jax.nn.log_sigmoid(-z), 0.0)
        rc = lax.cumsum(log_keep, axis=3, reverse=True)
        between = jnp.pad(rc[..., 1:], [(0, 0), (0, 0), (0, 0), (0, 1)])
        w = jnp.where(causal, jnp.exp(jax.nn.log_sigmoid(z) + between), 0.0)
        return jnp.einsum('bhqk,bkhd->bqhd', w.astype(v.dtype), v)

    return from_blocks(lax.map(one_block, (qb, pb)), tq)


def differential_attention(q, k, v, q_pos, k_pos, lam):
    scale = DF_QK_DIM ** -0.5
    slopes = 2.0 ** (-8.0 * jnp.arange(1, N_DF_HEADS + 1, dtype=jnp.float32) / N_DF_HEADS)
    qb, pb, tq = to_blocks(q, q_pos)

    def one_block(args):
        qi, pi = args
        s = jnp.einsum('bqhcd,bkhcd->bhcqk', qi, k).astype(jnp.float32) * scale
        dist = (pi[:, None] - k_pos[None, :]).astype(jnp.float32)
        s = s - slopes[None, :, None, None, None] * dist[None, None, None]
        s = jnp.where(dist >= 0, s, NEG_INF)
        p = jax.nn.softmax(s, axis=-1)
        a = p[:, :, 0] - lam * p[:, :, 1]
        return jnp.einsum('bhqk,bkhe->bqhe', a.astype(v.dtype), v)

    return from_blocks(lax.map(one_block, (qb, pb)), tq)


def layer(x, c, l, w_ada_l, b_ada_l, w_in_l, lam_l, subln_g_l, w_out_l, w_up_l, w_down_l,
          past, q_pos, k_pos):
    b, t = x.shape[0], x.shape[1]
    mod = (jax.nn.silu(c) @ w_ada_l + b_ada_l)[:, None, :]
    sh_a, sc_a, g_a, sh_m, sc_m, g_m = jnp.split(mod, N_MOD, axis=-1)
    h = (rms(x) * (1 + sc_a) + sh_a).astype(x.dtype)
    proj = h @ w_in_l
    cuts = np.cumsum([SB_WIDTH, SB_WIDTH, SB_WIDTH, DF_QK_WIDTH, DF_QK_WIDTH])
    q_sb, k_sb, v_sb, q_df, k_df, v_df = jnp.split(proj, cuts, axis=-1)
    q_sb = q_sb.reshape(b, t, N_SB_HEADS, SB_HEAD_DIM)
    k_sb = k_sb.reshape(b, t, N_SB_HEADS, SB_HEAD_DIM)
    v_sb = v_sb.reshape(b, t, N_SB_HEADS, SB_HEAD_DIM)
    q_df = q_df.reshape(b, t, N_DF_HEADS, 2, DF_QK_DIM)
    k_df = k_df.reshape(b, t, N_DF_HEADS, 2, DF_QK_DIM)
    v_df = v_df.reshape(b, t, N_DF_HEADS, DF_V_DIM)
    new_rows = (k_sb, v_sb, k_df, v_df)
    if past is None:
        ctx = new_rows
    else:
        ctx = tuple(jnp.concatenate([p, n], axis=1) for p, n in zip(past, new_rows))
    o_sb = stick_breaking_attention(q_sb, ctx[0], ctx[1], q_pos, k_pos)
    lambda_init = 0.8 - 0.6 * math.exp(-0.3 * l)
    lp = lam_l.astype(jnp.float32)
    lam = jnp.exp(jnp.sum(lp[0] * lp[1])) - jnp.exp(jnp.sum(lp[2] * lp[3])) + lambda_init
    o_df = differential_attention(q_df, ctx[2], ctx[3], q_pos, k_pos, lam)
    o_df = rms(o_df, SUBLN_EPS) * subln_g_l.astype(jnp.float32) * (1.0 - lambda_init)
    o = jnp.concatenate([o_sb.reshape(b, t, SB_WIDTH).astype(x.dtype),
                         o_df.reshape(b, t, DF_WIDTH).astype(x.dtype)], axis=-1)
    x = x + g_a * (o @ w_out_l)
    h = (rms(x) * (1 + sc_m) + sh_m).astype(x.dtype)
    u = jnp.square(jax.nn.relu(h @ w_up_l))
    x = x + g_m * (u @ w_down_l)
    return x, new_rows


def setup_inputs(seed: int = 0) -> dict:
    key = jax.random.key(seed)
    ks = jax.random.split(key, 20)
    n_pages = PAST_LEN // PAGE_SIZE
    n_used = DEC_BATCH * n_pages
    n_pool = (n_used * 5) // 4
    f32 = jnp.float32
    nrm = lambda k, s, sc=1.0: jax.random.normal(k, s, f32) * sc
    perm = jax.random.permutation(ks[6], n_pool)
    page_table = perm[:n_used].reshape(DEC_BATCH, n_pages).astype(jnp.int32)
    return {
        'x_prompt': nrm(ks[0], (BATCH, SEQ, D_MODEL)),
        'x_sample': nrm(ks[1], (DEC_BATCH, DEC_SEQ, D_MODEL)),
        'cache_sb_k': nrm(ks[2], (DEPTH, n_pool, PAGE_SIZE, N_SB_HEADS, SB_HEAD_DIM)),
        'cache_sb_v': nrm(ks[3], (DEPTH, n_pool, PAGE_SIZE, N_SB_HEADS, SB_HEAD_DIM)),
        'cache_df_k': nrm(ks[4], (DEPTH, n_pool, PAGE_SIZE, N_DF_HEADS, 2, DF_QK_DIM)),
        'cache_df_v': nrm(ks[5], (DEPTH, n_pool, PAGE_SIZE, N_DF_HEADS, DF_V_DIM)),
        'page_table': page_table,
        'c_prompt': nrm(ks[7], (BATCH, D_MODEL)),
        'c_sample': nrm(ks[8], (DEC_BATCH, D_MODEL)),
        'w_ada': nrm(ks[9], (DEPTH, D_MODEL, N_MOD * D_MODEL), 0.5 * D_MODEL ** -0.5),
        'b_ada': nrm(ks[10], (DEPTH, N_MOD * D_MODEL), 0.01),
        'w_in': nrm(ks[11], (DEPTH, D_MODEL, IN_WIDTH), D_MODEL ** -0.5),
        'diff_lambda': nrm(ks[12], (DEPTH, 4, DF_QK_DIM), 0.1),
        'diff_subln_g': 1.0 + nrm(ks[13], (DEPTH, DF_V_DIM), 0.02),
        'w_out': nrm(ks[14], (DEPTH, MIX_WIDTH, D_MODEL), MIX_WIDTH ** -0.5),
        'w_up': nrm(ks[15], (DEPTH, D_MODEL, D_FF), D_MODEL ** -0.5),
        'w_down': nrm(ks[16], (DEPTH, D_FF, D_MODEL), D_FF ** -0.5),
        'final_norm_g': 1.0 + nrm(ks[17], (D_MODEL,), 0.02),
    }


def reference(x_prompt, x_sample, cache_sb_k, cache_sb_v, cache_df_k, cache_df_v, page_table,
              c_prompt, c_sample, w_ada, b_ada, w_in, diff_lambda, diff_subln_g, w_out,
              w_up, w_down, final_norm_g):
    seq = x_prompt.shape[1]
    n_dec, n_pages = page_table.shape
    past_len = n_pages * PAGE_SIZE
    dec_seq = x_sample.shape[1]
    p_qpos = jnp.arange(seq, dtype=jnp.int32)
    s_qpos = past_len + jnp.arange(dec_seq, dtype=jnp.int32)
    s_kpos = jnp.arange(past_len + dec_seq, dtype=jnp.int32)

    xp, xs = x_prompt, x_sample
    rows_p = [[], [], [], []]
    rows_s = [[], [], [], []]
    for l in range(DEPTH):
        params = (w_ada[l], b_ada[l], w_in[l], diff_lambda[l], diff_subln_g[l],
                  w_out[l], w_up[l], w_down[l])
        xp, new_p = layer(xp, c_prompt, l, *params, None, p_qpos, p_qpos)
        past = (
            cache_sb_k[l, page_table].reshape(n_dec, past_len, N_SB_HEADS, SB_HEAD_DIM),
            cache_sb_v[l, page_table].reshape(n_dec, past_len, N_SB_HEADS, SB_HEAD_DIM),
            cache_df_k[l, page_table].reshape(n_dec, past_len, N_DF_HEADS, 2, DF_QK_DIM),
            cache_df_v[l, page_table].reshape(n_dec, past_len, N_DF_HEADS, DF_V_DIM),
        )
        xs, new_s = layer(xs, c_sample, l, *params, past, s_qpos, s_kpos)
        for i in range(4):
            rows_p[i].append(new_p[i])
            rows_s[i].append(new_s[i])

    y_prompt = (rms(xp) * final_norm_g.astype(jnp.float32)).astype(x_prompt.dtype)
    y_sample = (rms(xs) * final_norm_g.astype(jnp.float32)).astype(x_sample.dtype)
    sb_k_p, sb_v_p, df_k_p, df_v_p = [jnp.stack(r, axis=0) for r in rows_p]
    sb_k_s, sb_v_s, df_k_s, df_v_s = [jnp.stack(r, axis=0) for r in rows_s]
    return (y_prompt, y_sample, sb_k_p, sb_v_p, df_k_p, df_v_p, sb_k_s, sb_v_s, df_k_s, df_v_s)
```

```python
import functools
import math

import numpy as np
import jax
import jax.numpy as jnp
from jax import lax
from jax.experimental import pallas as pl
from jax.experimental.pallas import tpu as pltpu

F32 = jnp.float32
BF16 = jnp.bfloat16

D_MODEL = 1024
SB_HEAD_DIM = 64
N_SB_HEADS = 8
SB_WIDTH = N_SB_HEADS * SB_HEAD_DIM
DF_QK_DIM = 64
N_DF_HEADS = 4
DF_V_DIM = 2 * DF_QK_DIM
DF_WIDTH = N_DF_HEADS * DF_V_DIM
IN_WIDTH = 3 * SB_WIDTH + 3 * DF_WIDTH
D_FF = 4 * D_MODEL
N_MOD = 6
PAGE_SIZE = 128
NORM_EPS = 1e-6
SUBLN_EPS = 1e-5
NEG_INF = -1e30
LANES = 128
QK_SCALE = SB_HEAD_DIM ** -0.5
W = SB_WIDTH
assert DF_WIDTH == W

_C_QSB, _C_KSB, _C_VSB, _C_QDF, _C_KDF, _C_VDF = (g * W for g in range(6))

_VMEM_LIMIT = 56 * 1024 * 1024


def _params(sem, vmem=_VMEM_LIMIT):
    return pltpu.CompilerParams(dimension_semantics=sem, vmem_limit_bytes=vmem)


def _split2(x):
    hi = x.astype(BF16)
    lo = (x - hi.astype(F32)).astype(BF16)
    return hi, lo


def _dot(a, b):
    return jnp.dot(a, b, preferred_element_type=F32)


def _dot_nt(a, b):
    return lax.dot_general(a, b, (((1,), (1,)), ((), ())), preferred_element_type=F32)


def _rms(x, eps):
    return x * lax.rsqrt(jnp.mean(x * x, axis=-1, keepdims=True) + eps)


def _lambda_init(l):
    return 0.8 - 0.6 * math.exp(-0.3 * l)


def _lane_group_mask(shape, lo, hi):
    lane = lax.broadcasted_iota(jnp.int32, shape, len(shape) - 1)
    return (lane >= lo) & (lane < hi)


def _ada_kernel(c_ref, w_ref, b_ref, o_ref):
    c = c_ref[...]
    s = c / (1.0 + jnp.exp(-c))
    s_hi, s_lo = _split2(s)
    w_hi, w_lo = _split2(w_ref[0])
    acc = _dot(s_hi, w_hi) + _dot(s_hi, w_lo) + _dot(s_lo, w_hi)
    o_ref[0] = acc + b_ref[0]


def _ada_call(c_all, w_ada, b_ada):
    depth, d, n = w_ada.shape
    rows = c_all.shape[0]
    tn = 1024
    return pl.pallas_call(
        _ada_kernel,
        grid=(depth, n // tn),
        in_specs=[
            pl.BlockSpec((rows, d), lambda l, j: (0, 0)),
            pl.BlockSpec((1, d, tn), lambda l, j: (l, 0, j)),
            pl.BlockSpec((1, 1, tn), lambda l, j: (l, 0, j)),
        ],
        out_specs=pl.BlockSpec((1, rows, tn), lambda l, j: (l, 0, j)),
        out_shape=jax.ShapeDtypeStruct((depth, rows, n), F32),
        compiler_params=_params(("parallel", "parallel")),
        name="ada_mod",
    )(c_all, w_ada, b_ada.reshape(depth, 1, n))


def _modulated_norm(x_ref, sc_ref, sh_ref):
    return (_rms(x_ref[...], NORM_EPS) * (1.0 + sc_ref[0]) + sh_ref[0]).astype(BF16)


def _inproj_prompt_kernel(*refs, n_alias):
    x_ref, sc_ref, sh_ref, wq_ref, wt_ref, wv_ref = refs[:6]
    outs = refs[6 + n_alias:]
    q_ref, t32_refs, t16_refs, v32_ref, v16_ref = outs[0], outs[1:4], outs[4:7], outs[7], outs[8]
    hb = _modulated_norm(x_ref, sc_ref, sh_ref)
    for c in range(2):
        cols = slice(c * W, (c + 1) * W)
        q_ref[:, cols] = (_dot(hb, wq_ref[:, cols]) * QK_SCALE).astype(BF16)
    for g in range(3):
        r = _dot_nt(wt_ref[g * W:(g + 1) * W, :], hb)
        t32_refs[g][...] = r
        t16_refs[g][...] = r.astype(BF16)
    r = _dot(hb, wv_ref[...])
    v32_ref[...] = r
    v16_ref[...] = r.astype(BF16)


def _inproj_prompt_call(x, sc, sh, wq, wt, wv, layer, depth, batch, stacked):
    n, d = x.shape
    seq = n // batch
    tm = 512
    tpb = seq // tm
    mod_spec = pl.BlockSpec((1, 1, d), lambda i: (i // tpb, 0, 0))
    full = lambda a: pl.BlockSpec(a.shape, lambda i: (0,) * a.ndim)
    stacked_spec = pl.BlockSpec((None, None, W, tm), lambda i: (layer, i // tpb, 0, i % tpb))
    t16_spec = pl.BlockSpec((None, W, tm), lambda i: (i // tpb, 0, i % tpb))
    row_spec = lambda w: pl.BlockSpec((tm, w), lambda i: (i, 0))
    n_alias = 0 if stacked is None else 3
    in_specs = [row_spec(d), mod_spec, mod_spec, full(wq), full(wt), full(wv)]
    args = [x, sc, sh, wq, wt, wv]
    aliases = {}
    if stacked is not None:
        in_specs += [pl.BlockSpec(memory_space=pl.ANY)] * 3
        args += list(stacked)
        aliases = {6 + k: 1 + k for k in range(3)}
    stacked_shape = jax.ShapeDtypeStruct((depth, batch, W, seq), F32)
    t16_shape = jax.ShapeDtypeStruct((batch, W, seq), BF16)
    outs = pl.pallas_call(
        functools.partial(_inproj_prompt_kernel, n_alias=n_alias),
        grid=(n // tm,),
        in_specs=in_specs,
        out_specs=[row_spec(2 * W)] + [stacked_spec] * 3 + [t16_spec] * 3 + [row_spec(W), row_spec(W)],
        out_shape=[jax.ShapeDtypeStruct((n, 2 * W), BF16)] + [stacked_shape] * 3 + [t16_shape] * 3
        + [jax.ShapeDtypeStruct((n, W), F32), jax.ShapeDtypeStruct((n, W), BF16)],
        input_output_aliases=aliases,
        compiler_params=_params(("arbitrary",)),
        name="inproj_prompt",
    )(*args)
    return outs[0], outs[1:4], outs[4:7], outs[7], outs[8]


def _inproj_sample_kernel(x_ref, sc_ref, sh_ref, w_ref, pb_ref, ksb_ref, vsb_ref, kdf_ref, vdf_ref):
    hb = _modulated_norm(x_ref, sc_ref, sh_ref)
    f32_outs = {1: ksb_ref, 2: vsb_ref, 4: kdf_ref, 5: vdf_ref}
    for c in range(IN_WIDTH // W):
        cols = slice(c * W, (c + 1) * W)
        r = _dot(hb, w_ref[:, cols])
        if c in f32_outs:
            f32_outs[c][...] = r
            pb_ref[:, cols] = r.astype(BF16)
        else:
            pb_ref[:, cols] = (r * QK_SCALE).astype(BF16)


def _inproj_sample_call(x, sc, sh, w_in_b):
    n, d = x.shape
    full = lambda shape: pl.BlockSpec(shape, lambda i: (0,) * len(shape))
    kv_shape = jax.ShapeDtypeStruct((n, W), F32)
    return pl.pallas_call(
        _inproj_sample_kernel,
        grid=(1,),
        in_specs=[full((n, d)), full((1, n, d)), full((1, n, d)), full((d, IN_WIDTH))],
        out_specs=[full((n, IN_WIDTH))] + [full((n, W))] * 4,
        out_shape=[jax.ShapeDtypeStruct((n, IN_WIDTH), BF16)] + [kv_shape] * 4,
        compiler_params=_params(("arbitrary",)),
        name="inproj_sample",
    )(x, sc, sh, w_in_b)


def _suffix_matrix(tk):
    j = lax.broadcasted_iota(jnp.int32, (2 * tk, tk), 0)
    s = lax.broadcasted_iota(jnp.int32, (2 * tk, tk), 1)
    j = jnp.where(j >= tk, j - tk, j)
    return jnp.where(j > s, 1.0, 0.0).astype(BF16)


def _sb_weights(z, suffix, carry, valid):
    lk = -(jnp.maximum(z, 0.0) + jnp.log(1.0 + jnp.exp(-jnp.abs(z))))
    if valid is not None:
        lk = jnp.where(valid, lk, 0.0)
    hi, lo = _split2(lk)
    between = _dot(jnp.concatenate([hi, lo], axis=1), suffix)
    w = jnp.exp(z + lk + between + carry)
    if valid is not None:
        w = jnp.where(valid, w, 0.0)
    return w.astype(BF16), carry + jnp.sum(lk, axis=1, keepdims=True)


def _sb_kernel(q_ref, k_ref, v_ref, o_ref, *, tq):
    i = pl.program_id(2)
    q = q_ref[...]
    row = lax.broadcasted_iota(jnp.int32, (tq, tq), 0)
    col = lax.broadcasted_iota(jnp.int32, (tq, tq), 1)
    causal = col < row
    suffix = _suffix_matrix(tq)
    outs = []
    for hh in range(LANES // SB_HEAD_DIM):
        in_head = _lane_group_mask(q.shape, hh * SB_HEAD_DIM, (hh + 1) * SB_HEAD_DIM)
        qm = jnp.where(in_head, q, jnp.zeros_like(q))

        def block(kb, carry, acc, valid):
            start = pl.multiple_of(kb * tq, tq)
            kblk = k_ref[:, pl.ds(start, tq)]
            vblk = v_ref[:, pl.ds(start, tq)]
            w, carry = _sb_weights(_dot(qm, kblk), suffix, carry, valid)
            return carry, acc + _dot_nt(w, vblk)

        carry = jnp.zeros((tq, 1), F32)
        acc = jnp.zeros((tq, LANES), F32)
        carry, acc = block(i, carry, acc, causal)
        carry, acc = lax.fori_loop(
            0, i, lambda t, ca: block(i - 1 - t, ca[0], ca[1], None), (carry, acc))
        outs.append(acc)
    first = _lane_group_mask(outs[0].shape, 0, SB_HEAD_DIM)
    o_ref[...] = jnp.where(first, outs[0], outs[1]).astype(BF16)


def _sb_call(q, kt, vt, batch, seq, tq):
    n = batch * seq
    nq = seq // tq
    kv_spec = pl.BlockSpec((None, LANES, seq), lambda b, p, i: (b, p, 0))
    return pl.pallas_call(
        functools.partial(_sb_kernel, tq=tq),
        grid=(batch, W // LANES, nq),
        in_specs=[pl.BlockSpec((tq, LANES), lambda b, p, i: (b * nq + i, p)), kv_spec, kv_spec],
        out_specs=pl.BlockSpec((tq, LANES), lambda b, p, i: (b * nq + i, p)),
        out_shape=jax.ShapeDtypeStruct((n, W), BF16),
        compiler_params=_params(("parallel", "parallel", "arbitrary")),
        name="sb_attn",
    )(q, kt, vt)


def _lam_value(lam_ref, lambda_init):
    lp = lam_ref[...]
    a1 = jnp.sum(lp[0:1] * lp[1:2], axis=1, keepdims=True)
    a2 = jnp.sum(lp[2:3] * lp[3:4], axis=1, keepdims=True)
    return jnp.exp(a1) - jnp.exp(a2) + lambda_init


def _softmax_update(s, m, l, valid):
    if valid is not None:
        s = jnp.where(valid, s, NEG_INF)
    m_new = jnp.maximum(m, jnp.max(s, axis=1, keepdims=True))
    alpha = jnp.exp(m - m_new)
    p = jnp.exp(s - m_new)
    return p.astype(BF16), alpha, m_new, alpha * l + jnp.sum(p, axis=1, keepdims=True)


def _df_kernel(slopes_ref, lam_ref, g_ref, q_ref, k_ref, v_ref, o_ref, *, tq, lambda_init):
    h = pl.program_id(1)
    i = pl.program_id(2)
    slope = slopes_ref[h]
    q = q_ref[...]
    row = lax.broadcasted_iota(jnp.int32, (tq, tq), 0)
    col = lax.broadcasted_iota(jnp.int32, (tq, tq), 1)
    causal = col <= row
    bias0 = slope * (col - row).astype(F32)
    maps = []
    for c in range(2):
        in_comp = _lane_group_mask(q.shape, c * DF_QK_DIM, (c + 1) * DF_QK_DIM)
        qm = jnp.where(in_comp, q, jnp.zeros_like(q))

        def block(kb, m, l, acc, valid):
            start = pl.multiple_of(kb * tq, tq)
            kblk = k_ref[:, pl.ds(start, tq)]
            vblk = v_ref[pl.ds(start, tq), :]
            s = _dot(qm, kblk) + (bias0 - slope * ((i - kb) * tq).astype(F32))
            p, alpha, m, l = _softmax_update(s, m, l, valid)
            return m, l, alpha * acc + _dot(p, vblk)

        m = jnp.full((tq, 1), NEG_INF, F32)
        l = jnp.zeros((tq, 1), F32)
        acc = jnp.zeros((tq, LANES), F32)
        m, l, acc = block(i, m, l, acc, causal)
        m, l, acc = lax.fori_loop(
            0, i, lambda t, st: block(i - 1 - t, st[0], st[1], st[2], None), (m, l, acc))
        maps.append(acc / l)
    o = maps[0] - _lam_value(lam_ref, lambda_init) * maps[1]
    o = _rms(o, SUBLN_EPS) * g_ref[...] * (1.0 - lambda_init)
    o_ref[...] = o.astype(BF16)


def _alibi_slopes():
    return (2.0 ** (-8.0 * np.arange(1, N_DF_HEADS + 1, dtype=np.float32) / N_DF_HEADS)).astype(np.float32)


def _df_call(q, kt, v, lam_l, g_l, batch, seq, tq, lambda_init):
    n = batch * seq
    nq = seq // tq
    q_blocks = W // LANES
    return pl.pallas_call(
        functools.partial(_df_kernel, tq=tq, lambda_init=lambda_init),
        grid=(batch, N_DF_HEADS, nq),
        in_specs=[
            pl.BlockSpec(memory_space=pltpu.SMEM),
            pl.BlockSpec((4, DF_QK_DIM), lambda b, h, i: (0, 0)),
            pl.BlockSpec((1, DF_V_DIM), lambda b, h, i: (0, 0)),
            pl.BlockSpec((tq, LANES), lambda b, h, i: (b * nq + i, q_blocks + h)),
            pl.BlockSpec((None, LANES, seq), lambda b, h, i: (b, h, 0)),
            pl.BlockSpec((seq, LANES), lambda b, h, i: (b, h)),
        ],
        out_specs=pl.BlockSpec((tq, LANES), lambda b, h, i: (b * nq + i, h)),
        out_shape=jax.ShapeDtypeStruct((n, W), BF16),
        compiler_params=_params(("parallel", "parallel", "arbitrary")),
        name="df_attn",
    )(jnp.asarray(_alibi_slopes()), lam_l, g_l.reshape(1, DF_V_DIM), q, kt, v)


def _mlp_kernel(x_ref, osb_ref, odf_ref, ga_ref, scm_ref, shm_ref, gm_ref, wout_ref, wup_ref,
                wdn_ref, *rest, final):
    if final:
        fg_ref, o_ref = rest
    else:
        (o_ref,) = rest
    att = _dot(osb_ref[...].astype(BF16), wout_ref[0:W, :])
    att = att + _dot(odf_ref[...].astype(BF16), wout_ref[W:2 * W, :])
    x1 = x_ref[...] + ga_ref[0] * att
    hb = (_rms(x1, NORM_EPS) * (1.0 + scm_ref[0]) + shm_ref[0]).astype(BF16)
    acc = jnp.zeros_like(x1)
    for c in range(D_FF // D_MODEL):
        cols = slice(c * D_MODEL, (c + 1) * D_MODEL)
        u = jnp.maximum(_dot(hb, wup_ref[:, cols]), 0.0)
        acc = acc + _dot((u * u).astype(BF16), wdn_ref[cols, :])
    x2 = x1 + gm_ref[0] * acc
    if final:
        x2 = _rms(x2, NORM_EPS) * fg_ref[...]
    o_ref[...] = x2


def _mlp_call(x, osb, odf, ga, scm, shm, gm, wout_b, wup_b, wdn_b, tm, final_g=None):
    n, d = x.shape
    groups, rmod, _ = ga.shape
    tiles_per_group = n // tm // groups
    mod_spec = pl.BlockSpec((1, rmod, d), lambda i: (i // tiles_per_group, 0, 0))
    row_spec = lambda w: pl.BlockSpec((tm, w), lambda i: (i, 0))
    full_spec = lambda a: pl.BlockSpec(a.shape, lambda i: (0,) * a.ndim, pipeline_mode=pl.Buffered(1))
    in_specs = [row_spec(d), row_spec(W), row_spec(W), mod_spec, mod_spec, mod_spec,
                mod_spec, full_spec(wout_b), full_spec(wup_b), full_spec(wdn_b)]
    args = [x, osb, odf, ga, scm, shm, gm, wout_b, wup_b, wdn_b]
    if final_g is not None:
        in_specs.append(pl.BlockSpec((1, d), lambda i: (0, 0)))
        args.append(final_g.reshape(1, d))
    return pl.pallas_call(
        functools.partial(_mlp_kernel, final=final_g is not None),
        grid=(n // tm,),
        in_specs=in_specs,
        out_specs=row_spec(d),
        out_shape=jax.ShapeDtypeStruct((n, d), F32),
        compiler_params=_params(("arbitrary",)),
        name="out_mlp",
    )(*args)


_DEC_ROWS = 32
_SUB_PAGES = 2


def _page_specs(layer, n_pages, pages_per_step, block):
    def spec(p):
        def index_map(b, j, pt):
            return (layer, pt[b, n_pages - pages_per_step * (j + 1) + p]) + (0,) * (len(block) - 2)
        return pl.BlockSpec(block, index_map)
    return [spec(p) for p in range(pages_per_step)]


_T_PAGE = (None, None, W, PAGE_SIZE)
_V_PAGE = (None, None, PAGE_SIZE, N_DF_HEADS, DF_V_DIM)


def _load_t_block(page_refs, sub):
    pages = [page_refs[sub * _SUB_PAGES + s][...] for s in range(_SUB_PAGES)]
    return jnp.concatenate(pages, axis=1).astype(BF16)


def _pad_new_rows(new_ref, pad_ref):
    pad_ref[...] = jnp.zeros_like(pad_ref)
    pad_ref[0:new_ref.shape[1], :] = new_ref[0]
    return pad_ref[...].astype(BF16)


def _sbdec_kernel(pt_ref, qm_ref, kn_ref, vn_ref, *refs, pages_per_step, dec_seq):
    k_refs = refs[:pages_per_step]
    v_refs = refs[pages_per_step:2 * pages_per_step]
    o_ref, carry_ref, acc_ref, pad_ref = refs[2 * pages_per_step:]
    j = pl.program_id(1)
    qm = qm_ref[0]
    tk = _SUB_PAGES * PAGE_SIZE

    @pl.when(j == 0)
    def _():
        kn = _pad_new_rows(kn_ref, pad_ref)
        vn = _pad_new_rows(vn_ref, pad_ref)
        row = lax.broadcasted_iota(jnp.int32, (_DEC_ROWS, PAGE_SIZE), 0)
        col = lax.broadcasted_iota(jnp.int32, (_DEC_ROWS, PAGE_SIZE), 1)
        valid = col < row % dec_seq
        w, carry = _sb_weights(_dot_nt(qm, kn), _suffix_matrix(PAGE_SIZE),
                               jnp.zeros((_DEC_ROWS, 1), F32), valid)
        carry_ref[...] = carry
        acc_ref[...] = _dot(w, vn)

    suffix = _suffix_matrix(tk)
    carry = carry_ref[...]
    acc = acc_ref[...]
    for sub in reversed(range(pages_per_step // _SUB_PAGES)):
        w, carry = _sb_weights(_dot(qm, _load_t_block(k_refs, sub)), suffix, carry, None)
        acc = acc + _dot_nt(w, _load_t_block(v_refs, sub))
    carry_ref[...] = carry
    acc_ref[...] = acc

    @pl.when(j == pl.num_programs(1) - 1)
    def _():
        row = lax.broadcasted_iota(jnp.int32, (_DEC_ROWS, W), 0)
        lane = lax.broadcasted_iota(jnp.int32, (_DEC_ROWS, W), 1)
        acc_ref[...] = jnp.where(lane // SB_HEAD_DIM == row // dec_seq, acc, 0.0)
        out = acc_ref[0:dec_seq, :]
        for hh in range(1, N_SB_HEADS):
            out = out + acc_ref[hh * dec_seq:(hh + 1) * dec_seq, :]
        o_ref[0] = out


def _sbdec_call(page_table, qm, k_new, v_new, cache_kt, cache_vt, layer, pages_per_step):
    nb, dec_seq, _ = k_new.shape
    n_pages = page_table.shape[1]
    steps = n_pages // pages_per_step
    new_spec = pl.BlockSpec((1, dec_seq, W), lambda b, j, pt: (b, 0, 0))
    page_specs = _page_specs(layer, n_pages, pages_per_step, _T_PAGE)
    grid_spec = pltpu.PrefetchScalarGridSpec(
        num_scalar_prefetch=1,
        grid=(nb, steps),
        in_specs=[pl.BlockSpec((1, _DEC_ROWS, W), lambda b, j, pt: (b, 0, 0)), new_spec, new_spec]
        + page_specs + page_specs,
        out_specs=pl.BlockSpec((1, dec_seq, W), lambda b, j, pt: (b, 0, 0)),
        scratch_shapes=[pltpu.VMEM((_DEC_ROWS, 1), F32), pltpu.VMEM((_DEC_ROWS, W), F32),
                        pltpu.VMEM((PAGE_SIZE, W), F32)],
    )
    return pl.pallas_call(
        functools.partial(_sbdec_kernel, pages_per_step=pages_per_step, dec_seq=dec_seq),
        grid_spec=grid_spec,
        out_shape=jax.ShapeDtypeStruct((nb, dec_seq, W), F32),
        compiler_params=_params(("parallel", "arbitrary")),
        name="sb_decode",
    )(page_table, qm, k_new, v_new, *([cache_kt] * pages_per_step), *([cache_vt] * pages_per_step))


def _dfdec_kernel(pt_ref, lam_ref, g_ref, slope_ref, tq_ref, qm_ref, kn_ref, vn_ref, *refs,
                  pages_per_step, dec_seq, n_pages, lambda_init):
    k_refs = refs[:pages_per_step]
    v_refs = refs[pages_per_step:2 * pages_per_step]
    o_ref, m_ref, l_ref, acc_ref, pad_ref = refs[2 * pages_per_step:]
    j = pl.program_id(1)
    qm = qm_ref[0]
    slope = slope_ref[...]
    tquery = tq_ref[...]
    tk = _SUB_PAGES * PAGE_SIZE
    rows_per_head = _DEC_ROWS // N_DF_HEADS

    def weighted_values(p, value_of_head):
        parts = [_dot(p[hh * rows_per_head:(hh + 1) * rows_per_head], value_of_head(hh))
                 for hh in range(N_DF_HEADS)]
        return jnp.concatenate(parts, axis=0)

    @pl.when(j == 0)
    def _():
        kn = _pad_new_rows(kn_ref, pad_ref)
        vn = _pad_new_rows(vn_ref, pad_ref)
        col = lax.broadcasted_iota(jnp.int32, (_DEC_ROWS, PAGE_SIZE), 1).astype(F32)
        s = _dot_nt(qm, kn) + slope * (col - tquery)
        p, _, m, l = _softmax_update(s, jnp.full((_DEC_ROWS, 1), NEG_INF, F32),
                                     jnp.zeros((_DEC_ROWS, 1), F32), col <= tquery)
        m_ref[...] = m
        l_ref[...] = l
        acc_ref[...] = weighted_values(p, lambda hh: vn[:, hh * DF_V_DIM:(hh + 1) * DF_V_DIM])

    m = m_ref[...]
    l = l_ref[...]
    acc = acc_ref[...]
    col = lax.broadcasted_iota(jnp.int32, (_DEC_ROWS, tk), 1).astype(F32)
    first_page = n_pages - pages_per_step * (j + 1)
    for sub in reversed(range(pages_per_step // _SUB_PAGES)):
        base = ((first_page + sub * _SUB_PAGES - n_pages) * PAGE_SIZE).astype(F32)
        s = _dot(qm, _load_t_block(k_refs, sub)) + slope * (col + base - tquery)
        p, alpha, m, l = _softmax_update(s, m, l, None)

        def value_of_head(hh):
            pages = [v_refs[sub * _SUB_PAGES + t][:, hh, :] for t in range(_SUB_PAGES)]
            return jnp.concatenate(pages, axis=0).astype(BF16)

        acc = alpha * acc + weighted_values(p, value_of_head)
    m_ref[...] = m
    l_ref[...] = l
    acc_ref[...] = acc

    @pl.when(j == pl.num_programs(1) - 1)
    def _():
        acc_ref[...] = acc / l
        lam = _lam_value(lam_ref, lambda_init)
        heads = []
        for hh in range(N_DF_HEADS):
            r0 = hh * rows_per_head
            a = acc_ref[r0:r0 + dec_seq, :] - lam * acc_ref[r0 + dec_seq:r0 + 2 * dec_seq, :]
            heads.append(_rms(a, SUBLN_EPS) * g_ref[...] * (1.0 - lambda_init))
        o_ref[0] = jnp.concatenate(heads, axis=1)


def _dfdec_call(page_table, lam_l, g_l, qm, k_new, v_new, cache_kt, cache_v, layer, pages_per_step,
                lambda_init):
    nb, dec_seq, _ = k_new.shape
    n_pages = page_table.shape[1]
    steps = n_pages // pages_per_step
    rows = np.arange(_DEC_ROWS)
    rows_per_head = _DEC_ROWS // N_DF_HEADS
    slope_col = _alibi_slopes()[rows // rows_per_head].reshape(_DEC_ROWS, 1)
    tq_col = (rows % dec_seq).astype(np.float32).reshape(_DEC_ROWS, 1)
    const = lambda shape: pl.BlockSpec(shape, lambda b, j, pt: (0,) * len(shape))
    new_spec = pl.BlockSpec((1, dec_seq, W), lambda b, j, pt: (b, 0, 0))
    grid_spec = pltpu.PrefetchScalarGridSpec(
        num_scalar_prefetch=1,
        grid=(nb, steps),
        in_specs=[const((4, DF_QK_DIM)), const((1, DF_V_DIM)), const((_DEC_ROWS, 1)), const((_DEC_ROWS, 1)),
                  pl.BlockSpec((1, _DEC_ROWS, W), lambda b, j, pt: (b, 0, 0)), new_spec, new_spec]
        + _page_specs(layer, n_pages, pages_per_step, _T_PAGE)
        + _page_specs(layer, n_pages, pages_per_step, _V_PAGE),
        out_specs=pl.BlockSpec((1, dec_seq, W), lambda b, j, pt: (b, 0, 0)),
        scratch_shapes=[pltpu.VMEM((_DEC_ROWS, 1), F32), pltpu.VMEM((_DEC_ROWS, 1), F32),
                        pltpu.VMEM((_DEC_ROWS, DF_V_DIM), F32), pltpu.VMEM((PAGE_SIZE, W), F32)],
    )
    return pl.pallas_call(
        functools.partial(_dfdec_kernel, pages_per_step=pages_per_step, dec_seq=dec_seq,
                          n_pages=n_pages, lambda_init=lambda_init),
        grid_spec=grid_spec,
        out_shape=jax.ShapeDtypeStruct((nb, dec_seq, W), F32),
        compiler_params=_params(("parallel", "arbitrary")),
        name="df_decode",
    )(page_table, lam_l, g_l.reshape(1, DF_V_DIM), jnp.asarray(slope_col), jnp.asarray(tq_col), qm,
      k_new, v_new, *([cache_kt] * pages_per_step), *([cache_v] * pages_per_step))


def _masked_rows(q, n_groups, group_width):
    nb, t, width = q.shape
    lane_group = np.arange(width) // group_width
    mask = jnp.asarray(lane_group[None, :] == np.arange(n_groups)[:, None], q.dtype)
    return (q[:, None, :, :] * mask[None, :, None, :]).reshape(nb, n_groups * t, width)


def _feature_major_pages(cache):
    depth, pool, page = cache.shape[:3]
    nd = cache.ndim
    return jnp.transpose(cache, (0, 1) + tuple(range(3, nd)) + (2,)).reshape(depth, pool, W, page)


def _position_major(stacked, feature_shape):
    depth, batch, _, seq = stacked.shape
    nf = len(feature_shape)
    x = stacked.reshape((depth, batch) + feature_shape + (seq,))
    return jnp.transpose(x, (0, 1, 2 + nf) + tuple(range(2, 2 + nf)))


def kernel(x_prompt, x_sample, cache_sb_k, cache_sb_v, cache_df_k, cache_df_v, page_table, c_prompt,
           c_sample, w_ada, b_ada, w_in, diff_lambda, diff_subln_g, w_out, w_up, w_down, final_norm_g):
    batch, seq, d = x_prompt.shape
    nb, dec_seq, _ = x_sample.shape
    depth = w_in.shape[0]
    n_pages = page_table.shape[1]
    assert d == D_MODEL and nb * dec_seq == 128 and _DEC_ROWS == N_SB_HEADS * dec_seq

    tq = 256
    pages_per_step = 8
    assert seq % 512 == 0 and n_pages % pages_per_step == 0

    n_c = batch + nb
    rows_c = -(-n_c // 8) * 8
    c_all = jnp.concatenate([c_prompt, c_sample, jnp.zeros((rows_c - n_c, d), F32)], axis=0)
    mod = _ada_call(c_all, w_ada, b_ada)

    w_in_b, w_out_b = w_in.astype(BF16), w_out.astype(BF16)
    w_up_b, w_down_b = w_up.astype(BF16), w_down.astype(BF16)
    wq = jnp.concatenate([w_in_b[:, :, _C_QSB:_C_QSB + W], w_in_b[:, :, _C_QDF:_C_QDF + W]], axis=2)
    wt = jnp.swapaxes(jnp.concatenate([w_in_b[:, :, _C_KSB:_C_VSB + W], w_in_b[:, :, _C_KDF:_C_KDF + W]],
                                      axis=2), 1, 2)
    wv = w_in_b[:, :, _C_VDF:_C_VDF + W]
    cache_sb_kt, cache_sb_vt, cache_df_kt = (_feature_major_pages(c) for c in (cache_sb_k, cache_sb_v, cache_df_k))

    xp = x_prompt.reshape(batch * seq, d)
    xs = x_sample.reshape(nb * dec_seq, d)
    stacked = None
    vdf_rows = []
    rows_s = [[], [], [], []]
    for l in range(depth):
        lam_init = _lambda_init(l)
        final_g = final_norm_g if l == depth - 1 else None
        mod_p = mod[l, :batch].reshape(batch, N_MOD, 1, d)
        mod_s = jnp.repeat(mod[l, batch:n_c], dec_seq, axis=0).reshape(1, nb * dec_seq, N_MOD, d)
        mp = [mod_p[:, k] for k in range(N_MOD)]
        ms = [mod_s[:, :, k] for k in range(N_MOD)]

        q, stacked, t16, vdf, vdf16 = _inproj_prompt_call(
            xp, mp[1], mp[0], wq[l], wt[l], wv[l], l, depth, batch, stacked)
        osb = _sb_call(q, t16[0], t16[1], batch, seq, tq)
        odf = _df_call(q, t16[2], vdf16, diff_lambda[l], diff_subln_g[l], batch, seq, tq, lam_init)
        xp = _mlp_call(xp, osb, odf, mp[2], mp[4], mp[3], mp[5], w_out_b[l], w_up_b[l], w_down_b[l],
                       512, final_g)
        vdf_rows.append(vdf)

        pbs, *new_s = _inproj_sample_call(xs, ms[1], ms[0], w_in_b[l])
        pbs3 = pbs.reshape(nb, dec_seq, IN_WIDTH)
        new3 = [a.reshape(nb, dec_seq, W) for a in new_s]
        qm_sb = _masked_rows(pbs3[:, :, _C_QSB:_C_QSB + W], N_SB_HEADS, SB_HEAD_DIM)
        qm_df = _masked_rows(pbs3[:, :, _C_QDF:_C_QDF + W], 2 * N_DF_HEADS, DF_QK_DIM)
        osb_s = _sbdec_call(page_table, qm_sb, new3[0], new3[1], cache_sb_kt, cache_sb_vt, l, pages_per_step)
        odf_s = _dfdec_call(page_table, diff_lambda[l], diff_subln_g[l], qm_df, new3[2], new3[3],
                            cache_df_kt, cache_df_v, l, pages_per_step, lam_init)
        xs = _mlp_call(xs, osb_s.reshape(nb * dec_seq, W), odf_s.reshape(nb * dec_seq, W),
                       ms[2], ms[4], ms[3], ms[5], w_out_b[l], w_up_b[l], w_down_b[l], nb * dec_seq, final_g)
        for r, a in zip(rows_s, new_s):
            r.append(a)

    y_prompt = xp.reshape(batch, seq, d)
    y_sample = xs.reshape(nb * dec_seq // dec_seq, dec_seq, d)
    sb_shape = (N_SB_HEADS, SB_HEAD_DIM)
    dfk_shape = (N_DF_HEADS, 2, DF_QK_DIM)
    dfv_shape = (N_DF_HEADS, DF_V_DIM)
    outs_p = [_position_major(stacked[0], sb_shape), _position_major(stacked[1], sb_shape),
              _position_major(stacked[2], dfk_shape),
              jnp.stack(vdf_rows, axis=0).reshape((depth, batch, seq) + dfv_shape)]
    outs_s = [jnp.stack(r, axis=0).reshape((depth, nb, dec_seq) + s)
              for r, s in zip(rows_s, (sb_shape, sb_shape, dfk_shape, dfv_shape))]
    return (y_prompt, y_sample, *outs_p, *outs_s)
```
